```python
import math
import numpy as np
import jax
import jax.numpy as jnp
from jax import lax

D_MODEL = 1024
BATCH = 8
SEQ = 8192
DEPTH = 1
DEC_BATCH = 1
DEC_SEQ = 16384
PAST_LEN = 128

GRID_W = 64
HEAD_DIM = 128
N_Q_HEADS = 8
N_KV_HEADS = 2
ATTN_W = N_Q_HEADS * HEAD_DIM
KV_W = N_KV_HEADS * HEAD_DIM
Q_BLOCK = 128
ROPE_THETA = 10000.0
SSM_W = 512
SSM_GROUP = 16
SSM_GROUPS = SSM_W // SSM_GROUP
SSM_STATE = 64
STEP_MIN = 0.001
STEP_MAX = 0.1
N_MEM = 256
X_HEADS = 4
X_HEAD_DIM = 128
X_W = X_HEADS * X_HEAD_DIM
N_BRANCH = 3
IN_SIZES = (ATTN_W, KV_W, KV_W, ATTN_W, SSM_W, SSM_W, X_W, X_W, N_BRANCH * D_MODEL)
IN_W = ATTN_W + 2 * KV_W + ATTN_W + 2 * SSM_W + 2 * X_W + N_BRANCH * D_MODEL
EPS = 1e-6
F32 = jnp.float32

kernel_name = 'hybrid_gqa_s5_memory_encoder'


def rmsnorm(x, g):
    xf = x.astype(F32)
    y = xf * lax.rsqrt(jnp.mean(xf * xf, axis=-1, keepdims=True) + EPS) * g.astype(F32)
    return y.astype(x.dtype)


def axial_rope(seq_len):
    rows = seq_len // GRID_W
    row = jnp.broadcast_to(jnp.arange(rows, dtype=F32)[:, None], (rows, GRID_W)).reshape(-1)
    col = jnp.broadcast_to(jnp.arange(GRID_W, dtype=F32)[None, :], (rows, GRID_W)).reshape(-1)
    n_pairs = HEAD_DIM // 4
    freqs = ROPE_THETA ** (-jnp.arange(n_pairs, dtype=F32) / n_pairs)
    ang = jnp.concatenate([row[:, None] * freqs, col[:, None] * freqs], axis=-1)
    return jnp.cos(ang), jnp.sin(ang)


def apply_rope(x, cos, sin):
    xr = x.reshape(x.shape[:-1] + (HEAD_DIM // 2, 2))
    x0, x1 = xr[..., 0], xr[..., 1]
    c = cos[None, :, None, :]
    s = sin[None, :, None, :]
    return jnp.stack([x0 * c - x1 * s, x0 * s + x1 * c], axis=-1).reshape(x.shape)


def blocked_gqa(q, k, v):
    bsz, seq_len = q.shape[:2]
    grp = N_Q_HEADS // N_KV_HEADS
    n_blocks = seq_len // Q_BLOCK
    qb = q.reshape(bsz, n_blocks, Q_BLOCK, N_KV_HEADS, grp, HEAD_DIM).transpose(1, 0, 2, 3, 4, 5)
    scale = HEAD_DIM ** -0.5

    def one_block(qi):
        s = jnp.einsum('bqkgd,bskd->bkgqs', qi, k) * scale
        p = jax.nn.softmax(s.astype(F32), axis=-1)
        return jnp.einsum('bkgqs,bskd->bqkgd', p, v)

    o = lax.map(one_block, qb)
    return o.transpose(1, 0, 2, 3, 4, 5).reshape(bsz, seq_len, ATTN_W)


def s5_scan(u_g, a_re, a_im, b_re, b_im, c_re, c_im, log_step, reverse):
    lam = lax.complex(jnp.minimum(a_re.astype(F32), -1e-4), a_im.astype(F32))
    step = jnp.exp(log_step.astype(F32))[:, None]
    lam_bar = jnp.exp(lam * step)
    b_mat = lax.complex(b_re.astype(F32), b_im.astype(F32))
    b_bar = ((lam_bar - 1.0) / lam)[..., None] * b_mat
    bu = jnp.einsum('gnc,blgc->blgn', b_bar, u_g.astype(jnp.complex64))
    a = jnp.broadcast_to(lam_bar, bu.shape)

    def combine(left, right):
        a1, b1 = left
        a2, b2 = right
        return a2 * a1, a2 * b1 + b2

    _, states = lax.associative_scan(combine, (a, bu), axis=1, reverse=reverse)
    c_mat = lax.complex(c_re.astype(F32), c_im.astype(F32))
    return jnp.einsum('gcn,blgn->blgc', c_mat, states).real


def memory_cross_attention(q_x, mem, norm_mem, w_mem_kv):
    bsz, seq_len, _ = q_x.shape
    n_mem = mem.shape[1]
    kv = (rmsnorm(mem, norm_mem) @ w_mem_kv).astype(F32)
    k_m, v_m = jnp.split(kv, 2, axis=-1)
    k_m = k_m.reshape(bsz, n_mem, X_HEADS, X_HEAD_DIM)
    v_m = v_m.reshape(bsz, n_mem, X_HEADS, X_HEAD_DIM)
    qh = q_x.astype(F32).reshape(bsz, seq_len, X_HEADS, X_HEAD_DIM)
    s = jnp.einsum('blhd,bmhd->bhlm', qh, k_m) * (X_HEAD_DIM ** -0.5)
    p = jax.nn.softmax(s, axis=-1)
    return jnp.einsum('bhlm,bmhd->blhd', p, v_m).reshape(bsz, seq_len, X_W)


def hybrid_layer(x, mem, norm_in, w_in, q_norm, k_norm, s5_a_re, s5_a_im, s5_b_re, s5_b_im,
                 s5_c_re, s5_c_im, s5_log_step, s5_d, w_glu, b_glu, norm_mem, w_mem_kv,
                 w_proj_attn, w_proj_ssm, w_proj_cross, w_out):
    bsz, seq_len, _ = x.shape
    h = rmsnorm(x, norm_in)
    z = h @ w_in
    split_at = np.cumsum(IN_SIZES)[:-1].tolist()
    q, k, v, gate_a, u_s, gate_s, q_x, gate_x, merge_logits = jnp.split(z, split_at, axis=-1)

    cos, sin = axial_rope(seq_len)
    qh = rmsnorm(q.astype(F32).reshape(bsz, seq_len, N_Q_HEADS, HEAD_DIM), q_norm)
    kh = rmsnorm(k.astype(F32).reshape(bsz, seq_len, N_KV_HEADS, HEAD_DIM), k_norm)
    qh = apply_rope(qh, cos, sin)
    kh = apply_rope(kh, cos, sin)
    vh = v.astype(F32).reshape(bsz, seq_len, N_KV_HEADS, HEAD_DIM)
    y_a = blocked_gqa(qh, kh, vh) * jax.nn.silu(gate_a.astype(F32))

    u = u_s.astype(F32)
    u_g = u.reshape(bsz, seq_len, SSM_GROUPS, SSM_GROUP)
    y_s = (s5_scan(u_g, s5_a_re[0], s5_a_im[0], s5_b_re[0], s5_b_im[0], s5_c_re[0], s5_c_im[0], s5_log_step[0], False)
           + s5_scan(u_g, s5_a_re[1], s5_a_im[1], s5_b_re[1], s5_b_im[1], s5_c_re[1], s5_c_im[1], s5_log_step[1], True))
    y_s = y_s.reshape(bsz, seq_len, SSM_W) + s5_d.astype(F32) * u
    y_s = jax.nn.gelu(y_s)
    y_s = y_s * jax.nn.sigmoid(y_s @ w_glu.astype(F32) + b_glu.astype(F32))
    y_s = y_s * jax.nn.silu(gate_s.astype(F32))

    y_x = memory_cross_attention(q_x, mem, norm_mem, w_mem_kv) * jax.nn.silu(gate_x.astype(F32))

    g = jax.nn.sigmoid(merge_logits.astype(F32)).reshape(bsz, seq_len, N_BRANCH, D_MODEL)
    merged = (g[:, :, 0] * (y_a.astype(x.dtype) @ w_proj_attn)
              + g[:, :, 1] * (y_s.astype(x.dtype) @ w_proj_ssm)
              + g[:, :, 2] * (y_x.astype(x.dtype) @ w_proj_cross))
    return x + (merged.astype(x.dtype) @ w_out)


def trunk(x, mem, norm_in, w_in, q_norm, k_norm, s5_a_re, s5_a_im, s5_b_re, s5_b_im, s5_c_re,
          s5_c_im, s5_log_step, s5_d, w_glu, b_glu, norm_mem, w_mem_kv, w_proj_attn, w_proj_ssm,
          w_proj_cross, w_out, norm_final):
    for l in range(DEPTH):
        x = hybrid_layer(x, mem, norm_in[l], w_in[l], q_norm[l], k_norm[l], s5_a_re[l], s5_a_im[l],
                         s5_b_re[l], s5_b_im[l], s5_c_re[l], s5_c_im[l], s5_log_step[l], s5_d[l],
                         w_glu[l], b_glu[l], norm_mem[l], w_mem_kv[l], w_proj_attn[l], w_proj_ssm[l],
                         w_proj_cross[l], w_out[l])
    return rmsnorm(x, norm_final)


def setup_inputs(seed: int = 0) -> dict:
    key = jax.random.key(seed)
    ks = jax.random.split(key, 28)
    nrm = lambda k, shape, scale: jax.random.normal(k, shape, F32) * scale
    n_idx = jnp.arange(SSM_STATE, dtype=F32)
    s5_shape = (DEPTH, 2, SSM_GROUPS, SSM_STATE)
    return {
        'x_prompt': nrm(ks[0], (BATCH, SEQ, D_MODEL), 1.0),
        'x_sample': nrm(ks[1], (DEC_BATCH, DEC_SEQ, D_MODEL), 1.0),
        'mem_prompt': nrm(ks[2], (BATCH, N_MEM, D_MODEL), 1.0),
        'mem_sample': nrm(ks[3], (DEC_BATCH, N_MEM, D_MODEL), 1.0),
        'norm_in': 1.0 + nrm(ks[4], (DEPTH, D_MODEL), 0.02),
        'w_in': nrm(ks[5], (DEPTH, D_MODEL, IN_W), D_MODEL ** -0.5),
        'q_norm': 1.0 + nrm(ks[6], (DEPTH, HEAD_DIM), 0.02),
        'k_norm': 1.0 + nrm(ks[7], (DEPTH, HEAD_DIM), 0.02),
        's5_a_re': -0.5 + nrm(ks[8], s5_shape, 0.01),
        's5_a_im': math.pi * n_idx + nrm(ks[9], s5_shape, 0.01),
        's5_b_re': nrm(ks[10], (DEPTH, 2, SSM_GROUPS, SSM_STATE, SSM_GROUP), (2.0 * SSM_GROUP) ** -0.5),
        's5_b_im': nrm(ks[11], (DEPTH, 2, SSM_GROUPS, SSM_STATE, SSM_GROUP), (2.0 * SSM_GROUP) ** -0.5),
        's5_c_re': nrm(ks[12], (DEPTH, 2, SSM_GROUPS, SSM_GROUP, SSM_STATE), (2.0 * SSM_STATE) ** -0.5),
        's5_c_im': nrm(ks[13], (DEPTH, 2, SSM_GROUPS, SSM_GROUP, SSM_STATE), (2.0 * SSM_STATE) ** -0.5),
        's5_log_step': jax.random.uniform(ks[14], (DEPTH, 2, SSM_GROUPS), F32, math.log(STEP_MIN), math.log(STEP_MAX)),
        's5_d': nrm(ks[15], (DEPTH, SSM_W), 1.0),
        'w_glu': nrm(ks[16], (DEPTH, SSM_W, SSM_W), SSM_W ** -0.5),
        'b_glu': nrm(ks[17], (DEPTH, SSM_W), 0.01),
        'norm_mem': 1.0 + nrm(ks[18], (DEPTH, D_MODEL), 0.02),
        'w_mem_kv': nrm(ks[19], (DEPTH, D_MODEL, 2 * X_W), D_MODEL ** -0.5),
        'w_proj_attn': nrm(ks[20], (DEPTH, ATTN_W, D_MODEL), ATTN_W ** -0.5),
        'w_proj_ssm': nrm(ks[21], (DEPTH, SSM_W, D_MODEL), SSM_W ** -0.5),
        'w_proj_cross': nrm(ks[22], (DEPTH, X_W, D_MODEL), X_W ** -0.5),
        'w_out': nrm(ks[23], (DEPTH, D_MODEL, D_MODEL), D_MODEL ** -0.5),
        'norm_final': 1.0 + nrm(ks[24], (D_MODEL,), 0.02),
    }


def reference(x_prompt, x_sample, mem_prompt, mem_sample, norm_in, w_in, q_norm, k_norm,
              s5_a_re, s5_a_im, s5_b_re, s5_b_im, s5_c_re, s5_c_im, s5_log_step, s5_d,
              w_glu, b_glu, norm_mem, w_mem_kv, w_proj_attn, w_proj_ssm, w_proj_cross,
              w_out, norm_final):
    weights = (norm_in, w_in, q_norm, k_norm, s5_a_re, s5_a_im, s5_b_re, s5_b_im, s5_c_re,
               s5_c_im, s5_log_step, s5_d, w_glu, b_glu, norm_mem, w_mem_kv, w_proj_attn,
               w_proj_ssm, w_proj_cross, w_out, norm_final)
    y_prompt = trunk(x_prompt, mem_prompt, *weights)
    y_sample = trunk(x_sample, mem_sample, *weights)
    return (y_prompt, y_sample)
```

```python
import functools
import math

import jax
import jax.numpy as jnp
from jax import lax
from jax.experimental import pallas as pl
from jax.experimental.pallas import tpu as pltpu

F32 = jnp.float32
BF16 = jnp.bfloat16

D_MODEL = 1024
GRID_W = 64
HEAD_DIM = 128
N_Q_HEADS = 8
N_KV_HEADS = 2
Q_PER_KV = N_Q_HEADS // N_KV_HEADS
ATTN_W = N_Q_HEADS * HEAD_DIM
KV_W = N_KV_HEADS * HEAD_DIM
ROPE_THETA = 10000.0
SSM_W = 512
SSM_GROUP = 16
SSM_GROUPS = SSM_W // SSM_GROUP
SSM_STATE = 64
N_MEM = 256
X_HEADS = 4
X_HEAD_DIM = 128
X_W = X_HEADS * X_HEAD_DIM
N_BRANCH = 3
EPS = 1e-6

LANES = 128
V7X_VMEM_BYTES = 64 * 1024 * 1024
VMEM_LIMIT = V7X_VMEM_BYTES - 8 * 1024 * 1024

CHUNK = LANES
CHUNK_W = SSM_GROUP * CHUNK
TOKEN_TILE = 512
ATTN_TQ = 256
ATTN_TK = 512
NEG_BIG = -1e30
LOG2E = math.log2(math.e)

_OFF = {}
_o = 0
for _name, _w in (("q", ATTN_W), ("k", KV_W), ("v", KV_W), ("gate_a", ATTN_W), ("u_s", SSM_W),
                  ("gate_s", SSM_W), ("q_x", X_W), ("gate_x", X_W), ("merge", N_BRANCH * D_MODEL)):
    _OFF[_name] = (_o, _o + _w)
    _o += _w


def _dot(a, b):
    return jnp.dot(a, b, preferred_element_type=F32)


def _dot_nt(a, b):
    return lax.dot_general(a, b, (((1,), (1,)), ((), ())), preferred_element_type=F32)


def _sigmoid(x):
    return 1.0 / (1.0 + jnp.exp(-x))


def _silu(x):
    return x * _sigmoid(x)


def _gelu_tanh(x):
    return 0.5 * x * (1.0 + jnp.tanh(math.sqrt(2.0 / math.pi) * (x + 0.044715 * (x * x * x))))


def _rms_rows(x, gain_row):
    return x * lax.rsqrt(jnp.mean(x * x, axis=-1, keepdims=True) + EPS) * gain_row


def _tile_lanes(x, n):
    return x if n == 1 else jnp.concatenate([x] * n, axis=1)


def _tile_rows(x, n):
    return x if n == 1 else jnp.concatenate([x] * n, axis=0)


def _params(sem):
    return pltpu.CompilerParams(dimension_semantics=sem, vmem_limit_bytes=VMEM_LIMIT)


def _const_spec(shape):
    nd = len(shape)
    return pl.BlockSpec(shape, lambda *_: (0,) * nd)


def _mem_kv_kernel(mem_ref, g_ref, w_ref, k_ref, v_ref):
    hm = _rms_rows(mem_ref[0], g_ref[...]).astype(BF16)
    kv = _dot(hm, w_ref[...])
    k_ref[0] = kv[:, :X_W].astype(BF16)
    v_ref[0] = kv[:, X_W:].astype(BF16)


def _mem_kv(mem, g_row, w_bf):
    b = mem.shape[0]
    return pl.pallas_call(
        _mem_kv_kernel,
        grid=(b,),
        in_specs=[pl.BlockSpec((1, N_MEM, D_MODEL), lambda i: (i, 0, 0)),
                  _const_spec((1, D_MODEL)), _const_spec((D_MODEL, 2 * X_W))],
        out_specs=[pl.BlockSpec((1, N_MEM, X_W), lambda i: (i, 0, 0))] * 2,
        out_shape=[jax.ShapeDtypeStruct((b, N_MEM, X_W), BF16)] * 2,
        compiler_params=_params(("arbitrary",)),
        name="mem_kv",
    )(mem, g_row, w_bf)


def _toeplitz_kernel(g_ref, lam_ref, m_ref, taps_ref):
    taps_ref[...] = jnp.dot(g_ref[0], lam_ref[0], preferred_element_type=F32,
                            precision=lax.Precision.HIGHEST)

    def body(c_in, carry):
        r0 = pl.multiple_of(c_in * CHUNK, CHUNK)
        for c_out in range(SSM_GROUP):
            row = taps_ref[pl.ds(c_in * SSM_GROUP + c_out, 1), :]
            blk = pltpu.roll(jnp.broadcast_to(row, (CHUNK, 2 * CHUNK)), 0, 1, stride=1, stride_axis=0)
            m_ref[0, pl.ds(r0, CHUNK), c_out * CHUNK:(c_out + 1) * CHUNK] = blk[:, :CHUNK].astype(BF16)
        return carry

    lax.fori_loop(0, SSM_GROUP, body, 0)


def _toeplitz(g_mat, lam_mat):
    return pl.pallas_call(
        _toeplitz_kernel,
        grid=(SSM_GROUPS,),
        in_specs=[pl.BlockSpec((1, 2 * CHUNK, 2 * CHUNK), lambda g: (g, 0, 0))] * 2,
        out_specs=pl.BlockSpec((1, CHUNK_W, CHUNK_W), lambda g: (g, 0, 0)),
        out_shape=jax.ShapeDtypeStruct((SSM_GROUPS, CHUNK_W, CHUNK_W), BF16),
        scratch_shapes=[pltpu.VMEM((2 * CHUNK, 2 * CHUNK), F32)],
        compiler_params=_params(("arbitrary",)),
        name="ssm_toeplitz",
    )(g_mat, lam_mat)


def _cmul(ar, ai, br, bi):
    return ar * br - ai * bi, ar * bi + ai * br


def _ssm_tables(a_re, a_im, b_re, b_im, c_re, c_im, log_step):
    lam_re = jnp.minimum(a_re.astype(F32), -1e-4)
    lam_im = a_im.astype(F32)
    step = jnp.exp(log_step.astype(F32))[..., None]
    ls_re, ls_im = lam_re * step, lam_im * step

    def power(d):
        d = d.astype(F32)
        mag = jnp.exp(ls_re[..., None] * d)
        ang = ls_im[..., None] * d
        return mag * jnp.cos(ang), mag * jnp.sin(ang)

    lb_re, lb_im = power(jnp.ones((1,), F32))
    lb_re, lb_im = lb_re[..., 0], lb_im[..., 0]
    den = lam_re * lam_re + lam_im * lam_im
    x_re, x_im = lb_re - 1.0, lb_im
    f_re = (x_re * lam_re + x_im * lam_im) / den
    f_im = (x_im * lam_re - x_re * lam_im) / den
    bb_re, bb_im = _cmul(f_re[..., None], f_im[..., None], b_re.astype(F32), b_im.astype(F32))
    cc_re, cc_im = c_re.astype(F32), c_im.astype(F32)

    z_re, z_im = _cmul(cc_re[:, :, None, :, :], cc_im[:, :, None, :, :],
                       jnp.swapaxes(bb_re, 2, 3)[:, :, :, None, :], jnp.swapaxes(bb_im, 2, 3)[:, :, :, None, :])
    z_re = z_re.reshape(2, SSM_GROUPS, SSM_GROUP * SSM_GROUP, SSM_STATE)
    z_im = z_im.reshape(2, SSM_GROUPS, SSM_GROUP * SSM_GROUP, SSM_STATE)
    g_mat = jnp.concatenate([z_re[0], -z_im[0], z_re[1], -z_im[1]], axis=-1)

    idx = jnp.arange(2 * CHUNK)
    pf_re, pf_im = power(idx)
    keep_f = (idx < CHUNK).astype(F32)
    pb_re, pb_im = power((2 * CHUNK - idx) % (2 * CHUNK))
    keep_b = ((idx == 0) | (idx > CHUNK)).astype(F32)
    lam_mat = jnp.concatenate([pf_re[0] * keep_f, pf_im[0] * keep_f,
                               pb_re[1] * keep_b, pb_im[1] * keep_b], axis=1)

    s_idx = jnp.arange(CHUNK)
    qf_re, qf_im = power(CHUNK - 1 - s_idx)
    qb_re, qb_im = power(s_idx)

    def in_mat(p_re, p_im, bre, bim):
        return _cmul(jnp.swapaxes(p_re, 1, 2)[:, None, :, :], jnp.swapaxes(p_im, 1, 2)[:, None, :, :],
                     jnp.transpose(bre, (0, 2, 1))[:, :, None, :], jnp.transpose(bim, (0, 2, 1))[:, :, None, :])

    qf = in_mat(qf_re[0], qf_im[0], bb_re[0], bb_im[0])
    qb = in_mat(qb_re[1], qb_im[1], bb_re[1], bb_im[1])
    q_mat = jnp.concatenate([qf[0], qb[0], qf[1], qb[1]], axis=-1).reshape(SSM_GROUPS, CHUNK_W, 4 * SSM_STATE)

    of_re, of_im = power(s_idx + 1)
    ob_re, ob_im = power(CHUNK - s_idx)

    def out_mat(p_re, p_im, cre, cim):
        return _cmul(jnp.swapaxes(cre, 1, 2)[:, :, :, None], jnp.swapaxes(cim, 1, 2)[:, :, :, None],
                     p_re[:, :, None, :], p_im[:, :, None, :])

    pf = out_mat(of_re[0], of_im[0], cc_re[0], cc_im[0])
    pb = out_mat(ob_re[1], ob_im[1], cc_re[1], cc_im[1])
    p_mat = jnp.concatenate([pf[0], pb[0], -pf[1], -pb[1]], axis=1).reshape(SSM_GROUPS, 4 * SSM_STATE, CHUNK_W)

    dc_re, dc_im = power(jnp.full((1,), CHUNK, F32))
    dec = jnp.stack([jnp.concatenate([dc_re[0, :, :, 0], dc_re[1, :, :, 0]], axis=-1),
                     jnp.concatenate([dc_im[0, :, :, 0], dc_im[1, :, :, 0]], axis=-1)], axis=1)
    return g_mat, lam_mat, q_mat.astype(BF16), p_mat.astype(BF16), dec


def _in_proj_kernel(x_ref, gin_ref, wn_ref, wt_ref, gq_ref, gk_ref, cost_ref, sint_ref, cc_ref, ss_ref,
                    qt_ref, k_ref, vt_ref, ga_ref, ut_ref, gst_ref, *, bb, tt):
    tm = bb * tt
    x = x_ref[...].reshape(tm, D_MODEL)
    h = _rms_rows(x, gin_ref[...]).astype(BF16)

    zn = _dot(h, wn_ref[...])
    cc = _tile_rows(cc_ref[...], bb)
    ss = _tile_rows(ss_ref[...], bb)
    for hh in range(N_KV_HEADS):
        kn = _rms_rows(zn[:, hh * HEAD_DIM:(hh + 1) * HEAD_DIM], gk_ref[...])
        kr = (kn * cc + pltpu.roll(kn, HEAD_DIM // 2, 1) * ss).astype(BF16)
        for b in range(bb):
            k_ref[b, :, hh * HEAD_DIM:(hh + 1) * HEAD_DIM] = kr[b * tt:(b + 1) * tt]
    ga = _silu(zn[:, KV_W:]).astype(BF16)
    for b in range(bb):
        ga_ref[b] = ga[b * tt:(b + 1) * tt]

    zt = _dot_nt(wt_ref[...], h)
    cos_t = _tile_lanes(cost_ref[...], bb)
    sin_t = _tile_lanes(sint_ref[...], bb)
    gq = _tile_lanes(gq_ref[...], tm // LANES)
    half = HEAD_DIM // 2
    for hh in range(N_Q_HEADS):
        qh = zt[hh * HEAD_DIM:(hh + 1) * HEAD_DIM]
        qn = qh * lax.rsqrt(jnp.mean(qh * qh, axis=0, keepdims=True) + EPS) * gq
        x0, x1 = qn[:half], qn[half:]
        qr = jnp.concatenate([x0 * cos_t - x1 * sin_t, x0 * sin_t + x1 * cos_t], axis=0).astype(BF16)
        for b in range(bb):
            qt_ref[b, hh * HEAD_DIM:(hh + 1) * HEAD_DIM, :] = qr[:, b * tt:(b + 1) * tt]
    vt = zt[ATTN_W:ATTN_W + KV_W].astype(BF16)
    for b in range(bb):
        vt_ref[b] = vt[:, b * tt:(b + 1) * tt]
    ut_ref[...] = zt[ATTN_W + KV_W:ATTN_W + KV_W + SSM_W]
    gst_ref[...] = _silu(zt[ATTN_W + KV_W + SSM_W:]).astype(BF16)


def _in_proj(x, consts, *, nbh, bb, tt):
    b_total, seq, _ = x.shape
    tm = bb * tt
    n_tiles = (b_total * seq) // tm
    tok = b_total * seq

    def tile_bt(i):
        return (i % nbh, i // nbh)

    in_specs = [
        pl.BlockSpec((bb, tt, D_MODEL), lambda i: (*tile_bt(i), 0)),
        _const_spec((1, D_MODEL)),
        _const_spec((D_MODEL, KV_W + ATTN_W)),
        _const_spec((ATTN_W + KV_W + 2 * SSM_W, D_MODEL)),
        _const_spec((HEAD_DIM, LANES)),
        _const_spec((1, HEAD_DIM)),
        pl.BlockSpec((HEAD_DIM // 2, tt), lambda i: (0, i // nbh)),
        pl.BlockSpec((HEAD_DIM // 2, tt), lambda i: (0, i // nbh)),
        pl.BlockSpec((tt, HEAD_DIM), lambda i: (i // nbh, 0)),
        pl.BlockSpec((tt, HEAD_DIM), lambda i: (i // nbh, 0)),
    ]
    out_specs = [
        pl.BlockSpec((bb, ATTN_W, tt), lambda i: (i % nbh, 0, i // nbh)),
        pl.BlockSpec((bb, tt, KV_W), lambda i: (*tile_bt(i), 0)),
        pl.BlockSpec((bb, KV_W, tt), lambda i: (i % nbh, 0, i // nbh)),
        pl.BlockSpec((bb, tt, ATTN_W), lambda i: (*tile_bt(i), 0)),
        pl.BlockSpec((SSM_W, tm), lambda i: (0, i)),
        pl.BlockSpec((SSM_W, tm), lambda i: (0, i)),
    ]
    out_shape = [
        jax.ShapeDtypeStruct((b_total, ATTN_W, seq), BF16),
        jax.ShapeDtypeStruct((b_total, seq, KV_W), BF16),
        jax.ShapeDtypeStruct((b_total, KV_W, seq), BF16),
        jax.ShapeDtypeStruct((b_total, seq, ATTN_W), BF16),
        jax.ShapeDtypeStruct((SSM_W, tok), F32),
        jax.ShapeDtypeStruct((SSM_W, tok), BF16),
    ]
    return pl.pallas_call(
        functools.partial(_in_proj_kernel, bb=bb, tt=tt),
        grid=(n_tiles,),
        in_specs=in_specs, out_specs=out_specs, out_shape=out_shape,
        compiler_params=_params(("arbitrary",)),
        name="in_proj",
    )(x, consts["g_in"], consts["w_nat"], consts["w_t"], consts["gq"], consts["gk"],
      consts["cos_t"][:, :seq], consts["sin_t"][:, :seq], consts["cc"][:seq], consts["ss"][:seq])


def _attn_kernel(qt_ref, k_ref, vt_ref, ga_ref, o_ref, m_ref, l_ref, acc_ref, *, seq, tq, tk):
    n = Q_PER_KV * tq
    q = jnp.concatenate([qt_ref[0, hh * HEAD_DIM:(hh + 1) * HEAD_DIM, :] for hh in range(Q_PER_KV)], axis=1)
    m_ref[...] = jnp.full((1, n), NEG_BIG, F32)
    l_ref[...] = jnp.zeros((1, n), F32)
    acc_ref[...] = jnp.zeros((HEAD_DIM, n), F32)

    def body(c, carry):
        k0 = pl.multiple_of(c * tk, tk)
        s = _dot(k_ref[0, pl.ds(k0, tk), :], q)
        m_old = m_ref[...]
        m_new = jnp.maximum(m_old, jnp.max(s, axis=0, keepdims=True))
        p = jnp.exp2(s - m_new)
        alpha = jnp.exp2(m_old - m_new)
        l_ref[...] = alpha * l_ref[...] + jnp.sum(p, axis=0, keepdims=True)
        acc_ref[...] = alpha * acc_ref[...] + _dot(vt_ref[0, :, pl.ds(k0, tk)], p.astype(BF16))
        m_ref[...] = m_new
        return carry

    lax.fori_loop(0, seq // tk, body, 0)
    o = acc_ref[...] / l_ref[...]
    for hh in range(Q_PER_KV):
        oh = o[:, hh * tq:(hh + 1) * tq].T
        gate = ga_ref[0, :, hh * HEAD_DIM:(hh + 1) * HEAD_DIM].astype(F32)
        o_ref[0, :, hh * HEAD_DIM:(hh + 1) * HEAD_DIM] = (oh * gate).astype(BF16)


def _attention(qt, k, vt, ga):
    b_total, seq, _ = k.shape
    tq, tk = ATTN_TQ, ATTN_TK
    group_w = Q_PER_KV * HEAD_DIM
    n = Q_PER_KV * tq
    return pl.pallas_call(
        functools.partial(_attn_kernel, seq=seq, tq=tq, tk=tk),
        grid=(b_total, N_KV_HEADS, seq // tq),
        in_specs=[
            pl.BlockSpec((1, group_w, tq), lambda b, g, i: (b, g, i)),
            pl.BlockSpec((1, seq, HEAD_DIM), lambda b, g, i: (b, 0, g)),
            pl.BlockSpec((1, HEAD_DIM, seq), lambda b, g, i: (b, g, 0)),
            pl.BlockSpec((1, tq, group_w), lambda b, g, i: (b, i, g)),
        ],
        out_specs=pl.BlockSpec((1, tq, group_w), lambda b, g, i: (b, i, g)),
        out_shape=jax.ShapeDtypeStruct((b_total, seq, ATTN_W), BF16),
        scratch_shapes=[pltpu.VMEM((1, n), F32), pltpu.VMEM((1, n), F32), pltpu.VMEM((HEAD_DIM, n), F32)],
        compiler_params=_params(("arbitrary", "arbitrary", "arbitrary")),
        name="attention",
    )(qt, k, vt, ga)


def _ssm_kernel(u_ref, m_ref, q_ref, p_ref, dec_ref, y_ref, e_ref, sf_ref, sb_ref, *, rows, nb):
    a = jnp.concatenate([u_ref[0, c] for c in range(SSM_GROUP)], axis=1).astype(BF16)
    e_ref[...] = _dot(a, q_ref[0])
    n_steps = rows // nb
    fwd = lax.broadcasted_iota(jnp.int32, (nb, LANES), 1) < SSM_STATE
    a_re = dec_ref[0, 0:1, :]
    a_im = dec_ref[0, 1:2, :]

    def step(i, carry):
        s_re, s_im = carry
        rf = pl.multiple_of(i * nb, nb)
        rb = pl.multiple_of((n_steps - 1 - i) * nb, nb)
        s_cat = jnp.concatenate([s_re, s_im], axis=1)
        sf_ref[pl.ds(rf, nb), :] = s_cat
        sb_ref[pl.ds(rb, nb), :] = s_cat
        e_f = e_ref[pl.ds(rf, nb), :]
        e_b = e_ref[pl.ds(rb, nb), :]
        e_re = jnp.where(fwd, e_f[:, :LANES], e_b[:, :LANES])
        e_im = jnp.where(fwd, e_f[:, LANES:], e_b[:, LANES:])
        return a_re * s_re - a_im * s_im + e_re, a_re * s_im + a_im * s_re + e_im

    zero = jnp.zeros((nb, LANES), F32)
    lax.fori_loop(0, n_steps, step, (zero, zero))
    fwd_all = lax.broadcasted_iota(jnp.int32, (rows, LANES), 1) < SSM_STATE
    sf = sf_ref[...]
    sb = sb_ref[...]
    s_all = jnp.concatenate([jnp.where(fwd_all, sf[:, :LANES], sb[:, :LANES]),
                             jnp.where(fwd_all, sf[:, LANES:], sb[:, LANES:])], axis=1).astype(BF16)
    y = _dot(a, m_ref[0]) + _dot(s_all, p_ref[0])
    for c in range(SSM_GROUP):
        y_ref[0, c] = y[:, c * CHUNK:(c + 1) * CHUNK]


def _ssm(u_t, m_mat, q_mat, p_mat, dec, *, nb):
    tok = u_t.shape[1]
    rows = tok // CHUNK
    u4 = u_t.reshape(SSM_GROUPS, SSM_GROUP, rows, CHUNK)
    y4 = pl.pallas_call(
        functools.partial(_ssm_kernel, rows=rows, nb=nb),
        grid=(SSM_GROUPS,),
        in_specs=[
            pl.BlockSpec((1, SSM_GROUP, rows, CHUNK), lambda g: (g, 0, 0, 0)),
            pl.BlockSpec((1, CHUNK_W, CHUNK_W), lambda g: (g, 0, 0)),
            pl.BlockSpec((1, CHUNK_W, 4 * SSM_STATE), lambda g: (g, 0, 0)),
            pl.BlockSpec((1, 4 * SSM_STATE, CHUNK_W), lambda g: (g, 0, 0)),
            pl.BlockSpec((1, 2, LANES), lambda g: (g, 0, 0)),
        ],
        out_specs=pl.BlockSpec((1, SSM_GROUP, rows, CHUNK), lambda g: (g, 0, 0, 0)),
        out_shape=jax.ShapeDtypeStruct((SSM_GROUPS, SSM_GROUP, rows, CHUNK), F32),
        scratch_shapes=[pltpu.VMEM((rows, 4 * SSM_STATE), F32)] * 3,
        compiler_params=_params(("arbitrary",)),
        name="ssm_scan",
    )(u4, m_mat, q_mat, p_mat, dec)
    return y4.reshape(SSM_W, tok)


def _out_kernel(x_ref, ya_ref, yt_ref, ut_ref, gst_ref, km_ref, vm_ref, gin_ref, w2_ref, wglt_ref, bgl_ref,
                d_ref, wpa_ref, wps_ref, wpc_ref, wo_ref, gfin_ref, o_ref, *, bb, tt):
    tm = bb * tt
    reps = tm // LANES
    x = x_ref[...].reshape(tm, D_MODEL)
    h = _rms_rows(x, gin_ref[...]).astype(BF16)

    def w2(name):
        lo, hi = _OFF[name]
        base = _OFF["q_x"][0]
        return w2_ref[:, lo - base:hi - base]

    m_lo = _OFF["merge"][0] - _OFF["q_x"][0]

    def merge_gate(br):
        return _sigmoid(_dot(h, w2_ref[:, m_lo + br * D_MODEL:m_lo + (br + 1) * D_MODEL]))

    ys = yt_ref[...] + _tile_lanes(d_ref[...], reps) * ut_ref[...]
    ys = _gelu_tanh(ys)
    ys = ys * _sigmoid(_dot(wglt_ref[...], ys.astype(BF16)) + _tile_lanes(bgl_ref[...], reps))
    ys = ys * gst_ref[...].astype(F32)
    merged = merge_gate(1) * _dot(ys.T.astype(BF16), wps_ref[...])

    ya = jnp.concatenate([ya_ref[b] for b in range(bb)], axis=0) if bb > 1 else ya_ref[0]
    merged = merged + merge_gate(0) * _dot(ya, wpa_ref[...])

    zq = _dot(h, w2("q_x")).astype(BF16)
    gx = _silu(_dot(h, w2("gate_x")))
    scale = X_HEAD_DIM ** -0.5
    rows = []
    for b in range(bb):
        heads = []
        for hh in range(X_HEADS):
            sl = slice(hh * X_HEAD_DIM, (hh + 1) * X_HEAD_DIM)
            s = _dot_nt(zq[b * tt:(b + 1) * tt, sl], km_ref[b, :, sl]) * scale
            s = s - jnp.max(s, axis=-1, keepdims=True)
            p = jnp.exp(s)
            p = p / jnp.sum(p, axis=-1, keepdims=True)
            heads.append(_dot(p.astype(BF16), vm_ref[b, :, sl]))
        rows.append(jnp.concatenate(heads, axis=1))
    yx = (jnp.concatenate(rows, axis=0) if bb > 1 else rows[0]) * gx
    merged = merged + merge_gate(2) * _dot(yx.astype(BF16), wpc_ref[...])

    out = x + _dot(merged.astype(BF16), wo_ref[...])
    out = _rms_rows(out, gfin_ref[...])
    for b in range(bb):
        o_ref[b] = out[b * tt:(b + 1) * tt]


def _out_proj(x, ya, y_t, u_t, gs_t, km, vm, consts, *, nbh, bb, tt):
    b_total, seq, _ = x.shape
    tm = bb * tt
    n_tiles = (b_total * seq) // tm
    w2_w = X_W * 2 + N_BRANCH * D_MODEL
    in_specs = [
        pl.BlockSpec((bb, tt, D_MODEL), lambda i: (i % nbh, i // nbh, 0)),
        pl.BlockSpec((bb, tt, ATTN_W), lambda i: (i % nbh, i // nbh, 0)),
        pl.BlockSpec((SSM_W, tm), lambda i: (0, i)),
        pl.BlockSpec((SSM_W, tm), lambda i: (0, i)),
        pl.BlockSpec((SSM_W, tm), lambda i: (0, i)),
        pl.BlockSpec((bb, N_MEM, X_W), lambda i: (i % nbh, 0, 0)),
        pl.BlockSpec((bb, N_MEM, X_W), lambda i: (i % nbh, 0, 0)),
        _const_spec((1, D_MODEL)),
        _const_spec((D_MODEL, w2_w)),
        _const_spec((SSM_W, SSM_W)),
        _const_spec((SSM_W, LANES)),
        _const_spec((SSM_W, LANES)),
        _const_spec((ATTN_W, D_MODEL)),
        _const_spec((SSM_W, D_MODEL)),
        _const_spec((X_W, D_MODEL)),
        _const_spec((D_MODEL, D_MODEL)),
        _const_spec((1, D_MODEL)),
    ]
    return pl.pallas_call(
        functools.partial(_out_kernel, bb=bb, tt=tt),
        grid=(n_tiles,),
        in_specs=in_specs,
        out_specs=pl.BlockSpec((bb, tt, D_MODEL), lambda i: (i % nbh, i // nbh, 0)),
        out_shape=jax.ShapeDtypeStruct((b_total, seq, D_MODEL), F32),
        compiler_params=_params(("arbitrary",)),
        name="out_proj",
    )(x, ya, y_t, u_t, gs_t, km, vm, consts["g_in"], consts["w2"], consts["wglu_t"], consts["bglu"],
      consts["d_skip"], consts["wpa"], consts["wps"], consts["wpc"], consts["wo"], consts["g_fin"])


def _rope_tables(seq_len):
    rows = seq_len // GRID_W
    row = jnp.broadcast_to(jnp.arange(rows, dtype=F32)[:, None], (rows, GRID_W)).reshape(-1)
    col = jnp.broadcast_to(jnp.arange(GRID_W, dtype=F32)[None, :], (rows, GRID_W)).reshape(-1)
    n_pairs = HEAD_DIM // 4
    freqs = ROPE_THETA ** (-jnp.arange(n_pairs, dtype=F32) / n_pairs)
    ang = jnp.concatenate([row[:, None] * freqs, col[:, None] * freqs], axis=-1)
    return jnp.cos(ang), jnp.sin(ang)


def _prepare(norm_in, w_in, q_norm, k_norm, s5_d, w_glu, b_glu, w_proj_attn, w_proj_ssm, w_proj_cross,
             w_out, norm_final, max_seq):
    perm = jnp.concatenate([jnp.arange(0, HEAD_DIM, 2), jnp.arange(1, HEAD_DIM, 2)])

    def cols(name):
        lo, hi = _OFF[name]
        return w_in[:, lo:hi]

    def head_perm(w, n_heads):
        return w.reshape(D_MODEL, n_heads, HEAD_DIM)[:, :, perm].reshape(D_MODEL, n_heads * HEAD_DIM)

    wq = head_perm(cols("q"), N_Q_HEADS)
    wk = head_perm(cols("k"), N_KV_HEADS)
    cos, sin = _rope_tables(max_seq)
    q_scale = HEAD_DIM ** -0.5 * LOG2E
    lo2 = _OFF["q_x"][0]
    return {
        "g_in": norm_in.astype(F32)[None, :],
        "g_fin": norm_final.astype(F32)[None, :],
        "w_nat": jnp.concatenate([wk, cols("gate_a")], axis=1).astype(BF16),
        "w_t": jnp.concatenate([wq, cols("v"), cols("u_s"), cols("gate_s")], axis=1).T.astype(BF16),
        "gq": jnp.broadcast_to((q_norm.astype(F32)[perm] * q_scale)[:, None], (HEAD_DIM, LANES)),
        "gk": k_norm.astype(F32)[perm][None, :],
        "cos_t": cos.T, "sin_t": sin.T,
        "cc": jnp.concatenate([cos, cos], axis=1), "ss": jnp.concatenate([-sin, sin], axis=1),
        "w2": w_in[:, lo2:].astype(BF16),
        "wglu_t": w_glu.astype(F32).T.astype(BF16),
        "bglu": jnp.broadcast_to(b_glu.astype(F32)[:, None], (SSM_W, LANES)),
        "d_skip": jnp.broadcast_to(s5_d.astype(F32)[:, None], (SSM_W, LANES)),
        "wpa": w_proj_attn.astype(BF16), "wps": w_proj_ssm.astype(BF16), "wpc": w_proj_cross.astype(BF16),
        "wo": w_out.astype(BF16),
    }


def _tile_plan(batch, seq):
    if batch == 1:
        return 1, 1, TOKEN_TILE
    bb = TOKEN_TILE // CHUNK
    assert batch % bb == 0
    return batch // bb, bb, CHUNK


def _layer(x, mem, consts, ssm_mats, norm_mem_row, w_mem_kv_bf):
    batch, seq, _ = x.shape
    nbh, bb, tt = _tile_plan(batch, seq)
    m_mat, q_mat, p_mat, dec = ssm_mats
    km, vm = _mem_kv(mem, norm_mem_row, w_mem_kv_bf)
    qt, k, vt, ga, u_t, gs_t = _in_proj(x, consts, nbh=nbh, bb=bb, tt=tt)
    ya = _attention(qt, k, vt, ga)
    y_t = _ssm(u_t, m_mat, q_mat, p_mat, dec, nb=batch)
    return _out_proj(x, ya, y_t, u_t, gs_t, km, vm, consts, nbh=nbh, bb=bb, tt=tt)


def kernel(x_prompt, x_sample, mem_prompt, mem_sample, norm_in, w_in, q_norm, k_norm, s5_a_re, s5_a_im,
           s5_b_re, s5_b_im, s5_c_re, s5_c_im, s5_log_step, s5_d, w_glu, b_glu, norm_mem, w_mem_kv,
           w_proj_attn, w_proj_ssm, w_proj_cross, w_out, norm_final):
    assert norm_in.shape[0] == 1, "single-layer trunk"
    max_seq = max(x_prompt.shape[1], x_sample.shape[1])
    consts = _prepare(norm_in[0], w_in[0], q_norm[0], k_norm[0], s5_d[0], w_glu[0], b_glu[0],
                      w_proj_attn[0], w_proj_ssm[0], w_proj_cross[0], w_out[0], norm_final, max_seq)
    g_mat, lam_mat, q_mat, p_mat, dec = _ssm_tables(s5_a_re[0], s5_a_im[0], s5_b_re[0], s5_b_im[0],
                                                    s5_c_re[0], s5_c_im[0], s5_log_step[0])
    ssm_mats = (_toeplitz(g_mat, lam_mat), q_mat, p_mat, dec)
    norm_mem_row = norm_mem[0].astype(F32)[None, :]
    w_mem_kv_bf = w_mem_kv[0].astype(BF16)
    y_prompt = _layer(x_prompt, mem_prompt, consts, ssm_mats, norm_mem_row, w_mem_kv_bf)
    y_sample = _layer(x_sample, mem_sample, consts, ssm_mats, norm_mem_row, w_mem_kv_bf)
    return (y_prompt, y_sample)
```
